```python
import math
import jax, jax.numpy as jnp
from jax import lax
import numpy as np

D_MODEL = 1024
BATCH = 16
SEQ = 2048
DEPTH = 4

CTX_LEN = 256
GRID_W = 64
ATTN_HEADS = 4
QK_DIM = 64
V_DIM = 2 * QK_DIM
ATTN_WIDTH = ATTN_HEADS * V_DIM
QK_WIDTH = ATTN_HEADS * 2 * QK_DIM
CHUNK = 128
SG_GROUPS = 4
SG_GROUP_DIM = 128
SG_WIDTH = SG_GROUPS * SG_GROUP_DIM
D_FF = ((8 * D_MODEL // 3 + 255) // 256) * 256
Q_BLOCK = 128
ROPE_THETA = 10000.0
EPS = 1e-6

K_OFF = 0
V_OFF = K_OFF + QK_WIDTH
Q_OFF = V_OFF + ATTN_WIDTH
U_OFF = Q_OFF + QK_WIDTH
SGV_OFF = U_OFF + SG_WIDTH
GATE_OFF = SGV_OFF + SG_WIDTH
IN_WIDTH = GATE_OFF + 2 * D_MODEL

kernel_name = 'hybrid_diffattn_spatialgate_dit_block'


def rms_norm(x, g):
    xf = x.astype(jnp.float32)
    y = xf * lax.rsqrt(jnp.mean(xf * xf, axis=-1, keepdims=True) + EPS)
    return (y * g.astype(jnp.float32)).astype(x.dtype)


def layer_norm(x, g):
    xf = x.astype(jnp.float32)
    mu = jnp.mean(xf, axis=-1, keepdims=True)
    var = jnp.mean(jnp.square(xf - mu), axis=-1, keepdims=True)
    return ((xf - mu) * lax.rsqrt(var + EPS) * g.astype(jnp.float32)).astype(x.dtype)


def modulate(h, shift, scale):
    return h * (1 + scale) + shift


def axial_rope_tables(n_tokens, dtype):
    rows = n_tokens // GRID_W
    row = jnp.broadcast_to(jnp.arange(rows)[:, None], (rows, GRID_W)).reshape(-1).astype(jnp.float32)
    col = jnp.broadcast_to(jnp.arange(GRID_W)[None, :], (rows, GRID_W)).reshape(-1).astype(jnp.float32)
    half = QK_DIM // 2
    inv = 1.0 / (ROPE_THETA ** (jnp.arange(0, half, 2, dtype=jnp.float32) / half))
    ang_r = row[:, None] * inv[None, :]
    ang_c = col[:, None] * inv[None, :]
    return (jnp.cos(ang_r).astype(dtype), jnp.sin(ang_r).astype(dtype),
            jnp.cos(ang_c).astype(dtype), jnp.sin(ang_c).astype(dtype))


def rotate(x, cos, sin):
    x1, x2 = jnp.split(x, 2, axis=-1)
    return jnp.concatenate([x1 * cos - x2 * sin, x2 * cos + x1 * sin], axis=-1)


def apply_rope2d(x, tables):
    cr, sr, cc, sc = [t[None, :, None, None, :] for t in tables]
    half = QK_DIM // 2
    return jnp.concatenate([rotate(x[..., :half], cr, sr), rotate(x[..., half:], cc, sc)], axis=-1)


def diff_softmax_attend(q, k, v, lam):
    s = jnp.einsum('bqhmd,bkhmd->bhmqk', q, k).astype(jnp.float32) * (QK_DIM ** -0.5)
    p = jax.nn.softmax(s, axis=-1)
    p = p[:, :, 0] - lam * p[:, :, 1]
    return jnp.einsum('bhqk,bkhd->bqhd', p.astype(v.dtype), v)


def latent_diff_attention(q, k_all, v_all, lam):
    B, S = q.shape[0], q.shape[1]
    nb = S // Q_BLOCK
    qb = q.reshape(B, nb, Q_BLOCK, ATTN_HEADS, 2, QK_DIM).swapaxes(0, 1)
    out = lax.map(lambda blk: diff_softmax_attend(blk, k_all, v_all, lam), qb)
    return out.swapaxes(0, 1).reshape(B, S, ATTN_HEADS, V_DIM)


def diff_head_out(o, subln_g, lam_init):
    B, T = o.shape[0], o.shape[1]
    return (rms_norm(o, subln_g) * (1.0 - lam_init)).reshape(B, T, ATTN_WIDTH)


def spatial_gating(u, v, norm_g, w_s, b_s):
    B, T = v.shape[0], v.shape[1]
    vc = layer_norm(v, norm_g).reshape(B, T // CHUNK, CHUNK, SG_GROUPS, SG_GROUP_DIM)
    mixed = jnp.einsum('gpq,bnqgd->bnpgd', w_s, vc) + b_s.T[None, None, :, :, None]
    return u * mixed.reshape(B, T, SG_WIDTH)


def merge_branches(attn, sg, gates, w_a, w_sg, w_o):
    ga, gb = jnp.split(jax.nn.sigmoid(gates), 2, axis=-1)
    return (ga * (attn @ w_a) + gb * (sg @ w_sg)) @ w_o


def swiglu(h, w_in, w_out):
    a, b = jnp.split(h @ w_in, 2, axis=-1)
    return (jax.nn.silu(a) * b) @ w_out


def qk_heads(p):
    return p.reshape(p.shape[0], p.shape[1], ATTN_HEADS, 2, QK_DIM)


def v_heads(p):
    return p.reshape(p.shape[0], p.shape[1], ATTN_HEADS, V_DIM)


def setup_inputs(seed: int = 0) -> dict:
    key = jax.random.key(seed)
    ks = jax.random.split(key, 24)
    f32 = jnp.float32
    nrm = lambda k, shape, s: jax.random.normal(k, shape, f32) * s
    return {
        'x': nrm(ks[0], (BATCH, SEQ, D_MODEL), 1.0),
        'c': nrm(ks[1], (BATCH, D_MODEL), 1.0),
        'ctx': nrm(ks[2], (BATCH, CTX_LEN, D_MODEL), 1.0),
        'c_ctx': nrm(ks[3], (D_MODEL,), 1.0),
        'ada_w': nrm(ks[4], (DEPTH, D_MODEL, 6 * D_MODEL), 0.5 * D_MODEL ** -0.5),
        'ada_b': nrm(ks[5], (DEPTH, 6 * D_MODEL), 0.02),
        'norm1_g': 1.0 + nrm(ks[6], (DEPTH, D_MODEL), 0.02),
        'w_in': nrm(ks[7], (DEPTH, D_MODEL, IN_WIDTH), D_MODEL ** -0.5),
        'lambda_q1': nrm(ks[8], (DEPTH, QK_DIM), 0.1),
        'lambda_k1': nrm(ks[9], (DEPTH, QK_DIM), 0.1),
        'lambda_q2': nrm(ks[10], (DEPTH, QK_DIM), 0.1),
        'lambda_k2': nrm(ks[11], (DEPTH, QK_DIM), 0.1),
        'subln_g': 1.0 + nrm(ks[12], (DEPTH, V_DIM), 0.02),
        'sg_norm_g': 1.0 + nrm(ks[13], (DEPTH, SG_WIDTH), 0.02),
        'sg_w': nrm(ks[14], (DEPTH, SG_GROUPS, CHUNK, CHUNK), CHUNK ** -0.5),
        'sg_b': 1.0 + nrm(ks[15], (DEPTH, SG_GROUPS, CHUNK), 0.01),
        'w_branch_attn': nrm(ks[16], (DEPTH, ATTN_WIDTH, D_MODEL), ATTN_WIDTH ** -0.5),
        'w_branch_sg': nrm(ks[17], (DEPTH, SG_WIDTH, D_MODEL), SG_WIDTH ** -0.5),
        'w_out': nrm(ks[18], (DEPTH, D_MODEL, D_MODEL), D_MODEL ** -0.5),
        'norm2_g': 1.0 + nrm(ks[19], (DEPTH, D_MODEL), 0.02),
        'w_ffn_in': nrm(ks[20], (DEPTH, D_MODEL, 2 * D_FF), D_MODEL ** -0.5),
        'w_ffn_out': nrm(ks[21], (DEPTH, D_FF, D_MODEL), D_FF ** -0.5),
        'final_g': 1.0 + nrm(ks[22], (D_MODEL,), 0.02),
    }


def reference(x, c, ctx, c_ctx, ada_w, ada_b, norm1_g, w_in, lambda_q1, lambda_k1,
              lambda_q2, lambda_k2, subln_g, sg_norm_g, sg_w, sg_b, w_branch_attn,
              w_branch_sg, w_out, norm2_g, w_ffn_in, w_ffn_out, final_g):
    S = x.shape[1]
    tables = axial_rope_tables(S, x.dtype)
    silu_c = jax.nn.silu(c)
    silu_cc = jax.nn.silu(c_ctx)
    x_l, x_c = x, ctx
    for i in range(DEPTH):
        last = i == DEPTH - 1
        mod_l = (silu_c @ ada_w[i] + ada_b[i])[:, None, :]
        mod_c = silu_cc @ ada_w[i] + ada_b[i]
        sh1_l, sc1_l, g1_l, sh2_l, sc2_l, g2_l = jnp.split(mod_l, 6, axis=-1)
        sh1_c, sc1_c, g1_c, sh2_c, sc2_c, g2_c = jnp.split(mod_c, 6, axis=-1)
        lam_init = 0.8 - 0.6 * math.exp(-0.3 * i)
        lam = (jnp.exp(jnp.sum(lambda_q1[i].astype(jnp.float32) * lambda_k1[i].astype(jnp.float32)))
               - jnp.exp(jnp.sum(lambda_q2[i].astype(jnp.float32) * lambda_k2[i].astype(jnp.float32)))
               + lam_init)

        h_l = modulate(rms_norm(x_l, norm1_g[i]), sh1_l, sc1_l)
        h_c = modulate(rms_norm(x_c, norm1_g[i]), sh1_c, sc1_c)
        p_l = h_l @ w_in[i]
        p_c = h_c @ (w_in[i][:, :Q_OFF] if last else w_in[i])

        k_c = qk_heads(p_c[..., K_OFF:V_OFF])
        v_c = v_heads(p_c[..., V_OFF:Q_OFF])
        k_l = apply_rope2d(qk_heads(p_l[..., K_OFF:V_OFF]), tables)
        v_l = v_heads(p_l[..., V_OFF:Q_OFF])
        q_l = apply_rope2d(qk_heads(p_l[..., Q_OFF:U_OFF]), tables)
        k_all = jnp.concatenate([k_c, k_l], axis=1)
        v_all = jnp.concatenate([v_c, v_l], axis=1)

        attn_l = diff_head_out(latent_diff_attention(q_l, k_all, v_all, lam), subln_g[i], lam_init)
        sg_l = spatial_gating(jax.nn.gelu(p_l[..., U_OFF:SGV_OFF]), jax.nn.gelu(p_l[..., SGV_OFF:GATE_OFF]),
                              sg_norm_g[i], sg_w[i], sg_b[i])
        mix_l = merge_branches(attn_l, sg_l, p_l[..., GATE_OFF:], w_branch_attn[i], w_branch_sg[i], w_out[i])
        x_l = x_l + g1_l * mix_l

        if not last:
            q_c = qk_heads(p_c[..., Q_OFF:U_OFF])
            attn_c = diff_head_out(diff_softmax_attend(q_c, k_c, v_c, lam), subln_g[i], lam_init)
            sg_c = spatial_gating(jax.nn.gelu(p_c[..., U_OFF:SGV_OFF]), jax.nn.gelu(p_c[..., SGV_OFF:GATE_OFF]),
                                  sg_norm_g[i], sg_w[i], sg_b[i])
            mix_c = merge_branches(attn_c, sg_c, p_c[..., GATE_OFF:], w_branch_attn[i], w_branch_sg[i], w_out[i])
            x_c = x_c + g1_c * mix_c

        f_l = swiglu(modulate(rms_norm(x_l, norm2_g[i]), sh2_l, sc2_l), w_ffn_in[i], w_ffn_out[i])
        x_l = x_l + g2_l * f_l
        if not last:
            f_c = swiglu(modulate(rms_norm(x_c, norm2_g[i]), sh2_c, sc2_c), w_ffn_in[i], w_ffn_out[i])
            x_c = x_c + g2_c * f_c

    return rms_norm(x_l, final_g)
```

```python
import functools
import math

import jax
import jax.numpy as jnp
from jax import lax
from jax.experimental import pallas as pl
from jax.experimental.pallas import tpu as pltpu

D_MODEL = 1024
DEPTH = 4
GRID_W = 64
ATTN_HEADS = 4
QK_DIM = 64
V_DIM = 2 * QK_DIM
ATTN_WIDTH = ATTN_HEADS * V_DIM
QK_WIDTH = ATTN_HEADS * 2 * QK_DIM
CHUNK = 128
SG_GROUPS = 4
SG_GROUP_DIM = 128
SG_WIDTH = SG_GROUPS * SG_GROUP_DIM
D_FF = ((8 * D_MODEL // 3 + 255) // 256) * 256
ROPE_THETA = 10000.0
EPS = 1e-6

K_OFF = 0
V_OFF = K_OFF + QK_WIDTH
Q_OFF = V_OFF + ATTN_WIDTH
U_OFF = Q_OFF + QK_WIDTH
SGV_OFF = U_OFF + SG_WIDTH
GATE_OFF = SGV_OFF + SG_WIDTH
IN_WIDTH = GATE_OFF + 2 * D_MODEL

LANES = 128
ROPE_HALF = QK_DIM // 4
VMEM_LIMIT_BYTES = 56 * 1024 * 1024

F32 = jnp.float32
BF16 = jnp.bfloat16


def _const_spec(shape):
    zeros = (0,) * len(shape)
    return pl.BlockSpec(shape, lambda *_: zeros, pipeline_mode=pl.Buffered(1))


def _params(n_grid_dims):
    return pltpu.CompilerParams(
        dimension_semantics=("arbitrary",) * n_grid_dims,
        vmem_limit_bytes=VMEM_LIMIT_BYTES)


def _rms(x, g):
    return x * lax.rsqrt(jnp.mean(x * x, axis=-1, keepdims=True) + EPS) * g


def _mod_kernel(c_ref, w_ref, b_ref, o_ref):
    sc = jax.nn.silu(c_ref[...]).astype(BF16)
    o_ref[0] = jnp.dot(sc, w_ref[0].astype(BF16), preferred_element_type=F32) + b_ref[0]


def _modulation(c_all, ada_w, ada_b):
    rows = c_all.shape[0]
    n_col = 6
    return pl.pallas_call(
        _mod_kernel,
        grid=(DEPTH, n_col),
        in_specs=[
            pl.BlockSpec((rows, D_MODEL), lambda i, j: (0, 0)),
            pl.BlockSpec((1, D_MODEL, D_MODEL), lambda i, j: (i, 0, j)),
            pl.BlockSpec((1, 1, D_MODEL), lambda i, j: (i, 0, j)),
        ],
        out_specs=pl.BlockSpec((1, rows, D_MODEL), lambda i, j: (i, 0, j)),
        out_shape=jax.ShapeDtypeStruct((DEPTH, rows, 6 * D_MODEL), F32),
        compiler_params=_params(2),
        name="adaln_modulation",
    )(c_all, ada_w, ada_b.reshape(DEPTH, 1, 6 * D_MODEL))


def _rope(p, cos, sin_lo, sin_hi):
    out = []
    for j in range(p.shape[1] // LANES):
        blk = p[:, j * LANES:(j + 1) * LANES]
        nxt = pltpu.roll(blk, LANES - ROPE_HALF, axis=1)
        prv = pltpu.roll(blk, ROPE_HALF, axis=1)
        out.append(blk * cos + nxt * sin_lo + prv * sin_hi)
    return jnp.concatenate(out, axis=1)


def _inproj_kernel(*refs, rope, kv_only):
    x_ref, mod_ref, g_ref, w_ref = refs[:4]
    refs = refs[4:]
    if rope:
        cos_ref, slo_ref, shi_ref = refs[:3]
        refs = refs[3:]
    if kv_only:
        kt_ref, v_ref = refs
    else:
        lng_ref, sgw_ref, sgb_ref, wsg_ref, q_ref, kt_ref, v_ref, ga_ref, sgp_ref, sg_scr = refs

    x = x_ref[0]
    m = mod_ref[0]
    shift, scale = m[:, 0:D_MODEL], m[:, D_MODEL:2 * D_MODEL]
    h = (_rms(x, g_ref[...]) * (1.0 + scale) + shift).astype(BF16)

    def proj(lo, hi):
        return jnp.dot(h, w_ref[:, lo:hi], preferred_element_type=F32)

    if rope:
        cos, slo, shi = cos_ref[...], slo_ref[...], shi_ref[...]

    pk = proj(K_OFF, V_OFF)
    if rope:
        pk = _rope(pk, cos, slo, shi)
    kt_ref[0] = pk.T.astype(BF16)
    v_ref[0] = proj(V_OFF, Q_OFF).astype(BF16)
    if kv_only:
        return

    pq = proj(Q_OFF, U_OFF)
    if rope:
        pq = _rope(pq, cos, slo, shi)
    q_ref[0] = (pq * (QK_DIM ** -0.5)).astype(BF16)

    u = jax.nn.gelu(proj(U_OFF, SGV_OFF))
    sv = jax.nn.gelu(proj(SGV_OFF, GATE_OFF))
    mu = jnp.mean(sv, axis=-1, keepdims=True)
    svc = sv - mu
    var = jnp.mean(svc * svc, axis=-1, keepdims=True)
    svn = (svc * lax.rsqrt(var + EPS) * lng_ref[...]).astype(BF16)
    sgb = sgb_ref[...]
    for c in range(x.shape[0] // CHUNK):
        rows = slice(c * CHUNK, (c + 1) * CHUNK)
        for g in range(SG_GROUPS):
            cols = slice(g * SG_GROUP_DIM, (g + 1) * SG_GROUP_DIM)
            mixed = jnp.dot(sgw_ref[g], svn[rows, cols], preferred_element_type=F32)
            mixed = mixed + sgb[:, g:g + 1]
            sg_scr[rows, cols] = (u[rows, cols] * mixed).astype(BF16)

    ga_ref[0] = jax.nn.sigmoid(proj(GATE_OFF, GATE_OFF + D_MODEL)).astype(BF16)
    gb = jax.nn.sigmoid(proj(GATE_OFF + D_MODEL, IN_WIDTH))
    sgp = jnp.dot(sg_scr[...], wsg_ref[...], preferred_element_type=F32)
    sgp_ref[0] = (gb * sgp).astype(BF16)


def _in_projection(x, mod, per_batch_mod, norm_g, w_in, rope_tables, sg_params, tm, kv_only):
    batch, seq, _ = x.shape
    rope = rope_tables is not None
    mod_map = (lambda s, b: (b, 0, 0)) if per_batch_mod else (lambda s, b: (0, 0, 0))
    w_cols = Q_OFF if kv_only else IN_WIDTH
    in_specs = [
        pl.BlockSpec((1, tm, D_MODEL), lambda s, b: (b, s, 0)),
        pl.BlockSpec((1, 1, 6 * D_MODEL), mod_map),
        _const_spec((1, D_MODEL)),
        _const_spec((D_MODEL, w_cols)),
    ]
    args = [x, mod, norm_g, w_in[:, :w_cols] if kv_only else w_in]
    if rope:
        in_specs += [pl.BlockSpec((tm, LANES), lambda s, b: (s, 0))] * 3
        args += list(rope_tables)
    tok = lambda width: pl.BlockSpec((1, tm, width), lambda s, b: (b, s, 0))
    tok_shape = lambda width: jax.ShapeDtypeStruct((batch, seq, width), BF16)
    kt_spec = pl.BlockSpec((1, QK_WIDTH, tm), lambda s, b: (b, 0, s))
    kt_shape = jax.ShapeDtypeStruct((batch, QK_WIDTH, seq), BF16)
    if kv_only:
        out_specs = [kt_spec, tok(ATTN_WIDTH)]
        out_shape = [kt_shape, tok_shape(ATTN_WIDTH)]
        scratch = []
    else:
        lng, sgw, sgb_t, wsg = sg_params
        in_specs += [
            _const_spec((1, SG_WIDTH)),
            _const_spec((SG_GROUPS, CHUNK, CHUNK)),
            _const_spec((CHUNK, SG_GROUPS)),
            _const_spec((SG_WIDTH, D_MODEL)),
        ]
        args += [lng, sgw, sgb_t, wsg]
        out_specs = [tok(QK_WIDTH), kt_spec, tok(ATTN_WIDTH), tok(D_MODEL), tok(D_MODEL)]
        out_shape = [tok_shape(QK_WIDTH), kt_shape, tok_shape(ATTN_WIDTH),
                     tok_shape(D_MODEL), tok_shape(D_MODEL)]
        scratch = [pltpu.VMEM((tm, SG_WIDTH), BF16)]
    return pl.pallas_call(
        functools.partial(_inproj_kernel, rope=rope, kv_only=kv_only),
        grid=(seq // tm, batch),
        in_specs=in_specs,
        out_specs=out_specs,
        out_shape=out_shape,
        scratch_shapes=scratch,
        compiler_params=_params(2),
        name="in_projection",
    )(*args)


def _attn_kernel(*refs, n_pieces, lam_init):
    q_ref = refs[0]
    kt_refs = refs[1:1 + n_pieces]
    v_refs = refs[1 + n_pieces:1 + 2 * n_pieces]
    (ga_ref, sgp_ref, x_ref, mod_ref, lq1_ref, lk1_ref, lq2_ref, lk2_ref,
     subg_ref, wa_ref, wo_ref, o_ref, attn_scr) = refs[1 + 2 * n_pieces:]

    tq = q_ref.shape[1]
    lam = (jnp.exp(jnp.sum(lq1_ref[...] * lk1_ref[...], axis=-1, keepdims=True))
           - jnp.exp(jnp.sum(lq2_ref[...] * lk2_ref[...], axis=-1, keepdims=True))
           + lam_init)
    first_map = lax.broadcasted_iota(jnp.int32, (1, V_DIM), 1) < QK_DIM
    subg = subg_ref[...]

    for hd in range(ATTN_HEADS):
        cols = slice(hd * V_DIM, (hd + 1) * V_DIM)
        qh = q_ref[0, :, cols]
        zero = jnp.zeros_like(qh)
        qs = jnp.concatenate([jnp.where(first_map, qh, zero), jnp.where(first_map, zero, qh)], axis=0)
        s = [jnp.dot(qs, kt[0, cols, :], preferred_element_type=F32) for kt in kt_refs]
        mx = functools.reduce(jnp.maximum, [jnp.max(p, axis=-1, keepdims=True) for p in s])
        e = [jnp.exp(p - mx) for p in s]
        den = functools.reduce(jnp.add, [jnp.sum(p, axis=-1, keepdims=True) for p in e])
        o = functools.reduce(jnp.add, [
            jnp.dot(p.astype(BF16), v[0, :, cols], preferred_element_type=F32)
            for p, v in zip(e, v_refs)])
        o = o / den
        oh = o[:tq] - lam * o[tq:]
        attn_scr[:, cols] = (_rms(oh, subg) * (1.0 - lam_init)).astype(BF16)

    a = jnp.dot(attn_scr[...], wa_ref[...], preferred_element_type=F32)
    merged = ga_ref[0].astype(F32) * a + sgp_ref[0].astype(F32)
    mix = jnp.dot(merged.astype(BF16), wo_ref[...], preferred_element_type=F32)
    gate = mod_ref[0][:, 2 * D_MODEL:3 * D_MODEL]
    o_ref[0] = x_ref[0] + gate * mix


def _attention(q, kts, vs, ga, sgp, x, mod, per_batch_mod, lam_vecs, subg, w_a, w_o, lam_init, tq):
    batch, seq, _ = x.shape
    n_pieces = len(kts)
    mod_map = (lambda b, s: (b, 0, 0)) if per_batch_mod else (lambda b, s: (0, 0, 0))
    tok = lambda width: pl.BlockSpec((1, tq, width), lambda b, s: (b, s, 0))
    in_specs = [tok(QK_WIDTH)]
    in_specs += [pl.BlockSpec((1, QK_WIDTH, kt.shape[2]), lambda b, s: (b, 0, 0)) for kt in kts]
    in_specs += [pl.BlockSpec((1, v.shape[1], ATTN_WIDTH), lambda b, s: (b, 0, 0)) for v in vs]
    in_specs += [tok(D_MODEL), tok(D_MODEL), tok(D_MODEL),
                 pl.BlockSpec((1, 1, 6 * D_MODEL), mod_map)]
    in_specs += [_const_spec((1, QK_DIM))] * 4
    in_specs += [_const_spec((1, V_DIM)), _const_spec((ATTN_WIDTH, D_MODEL)),
                 _const_spec((D_MODEL, D_MODEL))]
    return pl.pallas_call(
        functools.partial(_attn_kernel, n_pieces=n_pieces, lam_init=lam_init),
        grid=(batch, seq // tq),
        in_specs=in_specs,
        out_specs=tok(D_MODEL),
        out_shape=jax.ShapeDtypeStruct((batch, seq, D_MODEL), F32),
        scratch_shapes=[pltpu.VMEM((tq, ATTN_WIDTH), BF16)],
        compiler_params=_params(2),
        name="diff_attention_merge",
    )(q, *kts, *vs, ga, sgp, x, mod, *lam_vecs, subg, w_a, w_o)


def _ffn_kernel(*refs, final):
    if final:
        x_ref, mod_ref, g_ref, w1_ref, w2_ref, fg_ref, o_ref = refs
    else:
        x_ref, mod_ref, g_ref, w1_ref, w2_ref, o_ref = refs
    x = x_ref[0]
    m = mod_ref[0]
    shift, scale, gate = (m[:, 3 * D_MODEL:4 * D_MODEL], m[:, 4 * D_MODEL:5 * D_MODEL],
                          m[:, 5 * D_MODEL:6 * D_MODEL])
    h = (_rms(x, g_ref[...]) * (1.0 + scale) + shift).astype(BF16)
    a = jnp.dot(h, w1_ref[:, :D_FF], preferred_element_type=F32)
    b = jnp.dot(h, w1_ref[:, D_FF:], preferred_element_type=F32)
    f = (jax.nn.silu(a) * b).astype(BF16)
    y = x + gate * jnp.dot(f, w2_ref[...], preferred_element_type=F32)
    if final:
        y = _rms(y, fg_ref[...])
    o_ref[0] = y


def _ffn(x, mod, per_batch_mod, norm_g, w1, w2, final_g, tm):
    batch, seq, _ = x.shape
    final = final_g is not None
    mod_map = (lambda b, s: (b, 0, 0)) if per_batch_mod else (lambda b, s: (0, 0, 0))
    tok = pl.BlockSpec((1, tm, D_MODEL), lambda b, s: (b, s, 0))
    in_specs = [tok, pl.BlockSpec((1, 1, 6 * D_MODEL), mod_map), _const_spec((1, D_MODEL)),
                _const_spec((D_MODEL, 2 * D_FF)), _const_spec((D_FF, D_MODEL))]
    args = [x, mod, norm_g, w1, w2]
    if final:
        in_specs.append(_const_spec((1, D_MODEL)))
        args.append(final_g)
    return pl.pallas_call(
        functools.partial(_ffn_kernel, final=final),
        grid=(batch, seq // tm),
        in_specs=in_specs,
        out_specs=tok,
        out_shape=jax.ShapeDtypeStruct((batch, seq, D_MODEL), F32),
        compiler_params=_params(2),
        name="swiglu_ffn",
    )(*args)


def _rope_tables(n_tokens):
    pos = jnp.arange(n_tokens)
    row = (pos // GRID_W).astype(F32)
    col = (pos % GRID_W).astype(F32)
    half = QK_DIM // 2
    inv = 1.0 / (ROPE_THETA ** (jnp.arange(0, half, 2, dtype=F32) / half))
    ang_r = row[:, None] * inv[None, :]
    ang_c = col[:, None] * inv[None, :]
    reps = LANES // QK_DIM
    cos = jnp.tile(jnp.concatenate([jnp.cos(ang_r)] * 2 + [jnp.cos(ang_c)] * 2, axis=1), (1, reps))
    sin = jnp.tile(jnp.concatenate([jnp.sin(ang_r)] * 2 + [jnp.sin(ang_c)] * 2, axis=1), (1, reps))
    first_half = (jnp.arange(LANES) % (2 * ROPE_HALF)) < ROPE_HALF
    sin_lo = jnp.where(first_half[None, :], -sin, 0.0)
    sin_hi = jnp.where(first_half[None, :], 0.0, sin)
    return cos, sin_lo, sin_hi


def kernel(x, c, ctx, c_ctx, ada_w, ada_b, norm1_g, w_in, lambda_q1, lambda_k1, lambda_q2,
           lambda_k2, subln_g, sg_norm_g, sg_w, sg_b, w_branch_attn, w_branch_sg, w_out,
           norm2_g, w_ffn_in, w_ffn_out, final_g):
    batch, seq, _ = x.shape
    ctx_len = ctx.shape[1]
    tm_lat, tq_lat = 512, 256
    tm_ctx = ctx_len

    pad = (-(batch + 1)) % 8
    c_all = jnp.concatenate([c, c_ctx[None, :], jnp.zeros((pad, D_MODEL), F32)], axis=0)
    mod_all = _modulation(c_all, ada_w, ada_b)
    tables = _rope_tables(seq)

    w_in_b = w_in.astype(BF16)
    sgw_b = sg_w.astype(BF16)
    wsg_b = w_branch_sg.astype(BF16)
    wa_b = w_branch_attn.astype(BF16)
    wo_b = w_out.astype(BF16)
    w1_b = w_ffn_in.astype(BF16)
    w2_b = w_ffn_out.astype(BF16)

    x_l, x_c = x, ctx
    for i in range(DEPTH):
        last = i == DEPTH - 1
        lam_init = 0.8 - 0.6 * math.exp(-0.3 * i)
        mod_l = mod_all[i, :batch].reshape(batch, 1, 6 * D_MODEL)
        mod_c = mod_all[i, batch:batch + 1].reshape(1, 1, 6 * D_MODEL)
        n1 = norm1_g[i].reshape(1, D_MODEL)
        n2 = norm2_g[i].reshape(1, D_MODEL)
        sg_params = (sg_norm_g[i].reshape(1, SG_WIDTH), sgw_b[i], sg_b[i].T, wsg_b[i])
        lam_vecs = [v[i].reshape(1, QK_DIM) for v in (lambda_q1, lambda_k1, lambda_q2, lambda_k2)]
        subg = subln_g[i].reshape(1, V_DIM)

        if last:
            kt_c, v_c = _in_projection(x_c, mod_c, False, n1, w_in_b[i], None, None, tm_ctx, True)
        else:
            q_c, kt_c, v_c, ga_c, sgp_c = _in_projection(
                x_c, mod_c, False, n1, w_in_b[i], None, sg_params, tm_ctx, False)
        q_l, kt_l, v_l, ga_l, sgp_l = _in_projection(
            x_l, mod_l, True, n1, w_in_b[i], tables, sg_params, tm_lat, False)

        x_l = _attention(q_l, [kt_c, kt_l], [v_c, v_l], ga_l, sgp_l, x_l, mod_l, True,
                         lam_vecs, subg, wa_b[i], wo_b[i], lam_init, tq_lat)
        x_l = _ffn(x_l, mod_l, True, n2, w1_b[i], w2_b[i],
                   final_g.reshape(1, D_MODEL) if last else None, tm_lat)
        if not last:
            x_c = _attention(q_c, [kt_c], [v_c], ga_c, sgp_c, x_c, mod_c, False,
                             lam_vecs, subg, wa_b[i], wo_b[i], lam_init, tm_ctx)
            x_c = _ffn(x_c, mod_c, False, n2, w1_b[i], w2_b[i], None, tm_ctx)
    return x_l
```

```python
import functools
import math

import jax
import jax.numpy as jnp
from jax import lax
from jax.experimental import pallas as pl
from jax.experimental.pallas import tpu as pltpu

D_MODEL = 1024
DEPTH = 4
GRID_W = 64
ATTN_HEADS = 4
QK_DIM = 64
V_DIM = 2 * QK_DIM
ATTN_WIDTH = ATTN_HEADS * V_DIM
QK_WIDTH = ATTN_HEADS * 2 * QK_DIM
CHUNK = 128
SG_GROUPS = 4
SG_GROUP_DIM = 128
SG_WIDTH = SG_GROUPS * SG_GROUP_DIM
D_FF = ((8 * D_MODEL // 3 + 255) // 256) * 256
ROPE_THETA = 10000.0
EPS = 1e-6

K_OFF = 0
V_OFF = K_OFF + QK_WIDTH
Q_OFF = V_OFF + ATTN_WIDTH
U_OFF = Q_OFF + QK_WIDTH
SGV_OFF = U_OFF + SG_WIDTH
GATE_OFF = SGV_OFF + SG_WIDTH
IN_WIDTH = GATE_OFF + 2 * D_MODEL

LANES = 128
ROPE_HALF = QK_DIM // 4
BF16_SUBLANES = 16
VT_HEAD_ROWS = V_DIM + BF16_SUBLANES
VT_ROWS = ATTN_HEADS * VT_HEAD_ROWS
VMEM_LIMIT_BYTES = 56 * 1024 * 1024

F32 = jnp.float32
BF16 = jnp.bfloat16


def _const_spec(shape):
    zeros = (0,) * len(shape)
    return pl.BlockSpec(shape, lambda *_: zeros, pipeline_mode=pl.Buffered(1))


def _params(n_grid_dims, flags=None):
    return pltpu.CompilerParams(
        dimension_semantics=("arbitrary",) * n_grid_dims,
        vmem_limit_bytes=VMEM_LIMIT_BYTES,
        flags=flags)


def _rms(x, g):
    return x * lax.rsqrt(jnp.mean(x * x, axis=-1, keepdims=True) + EPS) * g


def _mod_kernel(c_ref, w_ref, b_ref, o_ref):
    sc = jax.nn.silu(c_ref[...]).astype(BF16)
    o_ref[0] = jnp.dot(sc, w_ref[0].astype(BF16), preferred_element_type=F32) + b_ref[0]


def _modulation(c_all, ada_w, ada_b):
    rows = c_all.shape[0]
    n_col = 6
    return pl.pallas_call(
        _mod_kernel,
        grid=(DEPTH, n_col),
        in_specs=[
            pl.BlockSpec((rows, D_MODEL), lambda i, j: (0, 0)),
            pl.BlockSpec((1, D_MODEL, D_MODEL), lambda i, j: (i, 0, j)),
            pl.BlockSpec((1, 1, D_MODEL), lambda i, j: (i, 0, j)),
        ],
        out_specs=pl.BlockSpec((1, rows, D_MODEL), lambda i, j: (i, 0, j)),
        out_shape=jax.ShapeDtypeStruct((DEPTH, rows, 6 * D_MODEL), F32),
        compiler_params=_params(2),
        name="adaln_modulation",
    )(c_all, ada_w, ada_b.reshape(DEPTH, 1, 6 * D_MODEL))


def _rope(p, cos, sin_lo, sin_hi):
    out = []
    for j in range(p.shape[1] // LANES):
        blk = p[:, j * LANES:(j + 1) * LANES]
        nxt = pltpu.roll(blk, LANES - ROPE_HALF, axis=1)
        prv = pltpu.roll(blk, ROPE_HALF, axis=1)
        out.append(blk * cos + nxt * sin_lo + prv * sin_hi)
    return jnp.concatenate(out, axis=1)


def _inproj_kernel(*refs, rope, kv_only):
    x_ref, mod_ref, g_ref, w_ref = refs[:4]
    refs = refs[4:]
    if rope:
        cos_ref, slo_ref, shi_ref = refs[:3]
        refs = refs[3:]
    if kv_only:
        k_ref, vt_ref = refs
    else:
        lng_ref, sgw_ref, sgb_ref, wsg_ref, qt_ref, k_ref, vt_ref, ga_ref, sgp_ref, sg_scr = refs

    x = x_ref[0]
    m = mod_ref[0]
    shift, scale = m[:, 0:D_MODEL], m[:, D_MODEL:2 * D_MODEL]
    h = (_rms(x, g_ref[...]) * (1.0 + scale) + shift).astype(BF16)

    def proj(lo, hi):
        return jnp.dot(h, w_ref[:, lo:hi], preferred_element_type=F32)

    if rope:
        cos, slo, shi = cos_ref[...], slo_ref[...], shi_ref[...]

    pk = proj(K_OFF, V_OFF)
    if rope:
        pk = _rope(pk, cos, slo, shi)
    k_ref[0] = pk.astype(BF16)
    pvt = proj(V_OFF, Q_OFF).T
    ones = jnp.ones((VT_HEAD_ROWS - V_DIM, x.shape[0]), BF16)
    for hd in range(ATTN_HEADS):
        base = hd * VT_HEAD_ROWS
        vt_ref[0, base:base + V_DIM, :] = pvt[hd * V_DIM:(hd + 1) * V_DIM].astype(BF16)
        vt_ref[0, base + V_DIM:base + VT_HEAD_ROWS, :] = ones
    if kv_only:
        return

    pq = proj(Q_OFF, U_OFF)
    if rope:
        pq = _rope(pq, cos, slo, shi)
    qt_ref[0] = (pq * (QK_DIM ** -0.5)).T.astype(BF16)

    u = jax.nn.gelu(proj(U_OFF, SGV_OFF))
    sv = jax.nn.gelu(proj(SGV_OFF, GATE_OFF))
    mu = jnp.mean(sv, axis=-1, keepdims=True)
    svc = sv - mu
    var = jnp.mean(svc * svc, axis=-1, keepdims=True)
    svn = (svc * lax.rsqrt(var + EPS) * lng_ref[...]).astype(BF16)
    sgb = sgb_ref[...]
    for c in range(x.shape[0] // CHUNK):
        rows = slice(c * CHUNK, (c + 1) * CHUNK)
        for g in range(SG_GROUPS):
            cols = slice(g * SG_GROUP_DIM, (g + 1) * SG_GROUP_DIM)
            mixed = jnp.dot(sgw_ref[g], svn[rows, cols], preferred_element_type=F32)
            mixed = mixed + sgb[:, g:g + 1]
            sg_scr[rows, cols] = (u[rows, cols] * mixed).astype(BF16)

    ga_ref[0] = jax.nn.sigmoid(proj(GATE_OFF, GATE_OFF + D_MODEL)).astype(BF16)
    gb = jax.nn.sigmoid(proj(GATE_OFF + D_MODEL, IN_WIDTH))
    sgp = jnp.dot(sg_scr[...], wsg_ref[...], preferred_element_type=F32)
    sgp_ref[0] = (gb * sgp).astype(BF16)


def _in_projection(x, mod, per_batch_mod, norm_g, w_in, rope_tables, sg_params, tm, kv_only):
    batch, seq, _ = x.shape
    rope = rope_tables is not None
    mod_map = (lambda s, b: (b, 0, 0)) if per_batch_mod else (lambda s, b: (0, 0, 0))
    w_cols = Q_OFF if kv_only else IN_WIDTH
    in_specs = [
        pl.BlockSpec((1, tm, D_MODEL), lambda s, b: (b, s, 0)),
        pl.BlockSpec((1, 1, 6 * D_MODEL), mod_map),
        _const_spec((1, D_MODEL)),
        _const_spec((D_MODEL, w_cols)),
    ]
    args = [x, mod, norm_g, w_in[:, :w_cols] if kv_only else w_in]
    if rope:
        in_specs += [pl.BlockSpec((tm, LANES), lambda s, b: (s, 0))] * 3
        args += list(rope_tables)
    tok = lambda width: pl.BlockSpec((1, tm, width), lambda s, b: (b, s, 0))
    tok_shape = lambda width: jax.ShapeDtypeStruct((batch, seq, width), BF16)
    tr = lambda rows: pl.BlockSpec((1, rows, tm), lambda s, b: (b, 0, s))
    tr_shape = lambda rows: jax.ShapeDtypeStruct((batch, rows, seq), BF16)
    if kv_only:
        out_specs = [tok(QK_WIDTH), tr(VT_ROWS)]
        out_shape = [tok_shape(QK_WIDTH), tr_shape(VT_ROWS)]
        scratch = []
    else:
        lng, sgw, sgb_t, wsg = sg_params
        in_specs += [
            _const_spec((1, SG_WIDTH)),
            _const_spec((SG_GROUPS, CHUNK, CHUNK)),
            _const_spec((CHUNK, SG_GROUPS)),
            _const_spec((SG_WIDTH, D_MODEL)),
        ]
        args += [lng, sgw, sgb_t, wsg]
        out_specs = [tr(QK_WIDTH), tok(QK_WIDTH), tr(VT_ROWS), tok(D_MODEL), tok(D_MODEL)]
        out_shape = [tr_shape(QK_WIDTH), tok_shape(QK_WIDTH), tr_shape(VT_ROWS),
                     tok_shape(D_MODEL), tok_shape(D_MODEL)]
        scratch = [pltpu.VMEM((tm, SG_WIDTH), BF16)]
    return pl.pallas_call(
        functools.partial(_inproj_kernel, rope=rope, kv_only=kv_only),
        grid=(seq // tm, batch),
        in_specs=in_specs,
        out_specs=out_specs,
        out_shape=out_shape,
        scratch_shapes=scratch,
        compiler_params=_params(2),
        name="in_projection",
    )(*args)


def _attn_kernel(*refs, n_pieces, lam_init):
    qt_ref = refs[0]
    k_refs = refs[1:1 + n_pieces]
    vt_refs = refs[1 + n_pieces:1 + 2 * n_pieces]
    (ga_ref, sgp_ref, x_ref, mod_ref, lq1_ref, lk1_ref, lq2_ref, lk2_ref,
     subg_ref, wa_ref, wo_ref, o_ref, attn_scr) = refs[1 + 2 * n_pieces:]

    tq = qt_ref.shape[2]
    lam = (jnp.exp(jnp.sum(lq1_ref[...] * lk1_ref[...], axis=-1, keepdims=True))
           - jnp.exp(jnp.sum(lq2_ref[...] * lk2_ref[...], axis=-1, keepdims=True))
           + lam_init)
    first_map = lax.broadcasted_iota(jnp.int32, (V_DIM, 1), 0) < QK_DIM
    subg = subg_ref[...]

    def scores(hd):
        cols = slice(hd * V_DIM, (hd + 1) * V_DIM)
        qt = qt_ref[0, cols, :]
        zero = jnp.zeros_like(qt)
        w = jnp.concatenate([jnp.where(first_map, qt, zero), jnp.where(first_map, zero, qt)], axis=1)
        return [jnp.dot(k[0, :, cols], w, preferred_element_type=F32) for k in k_refs]

    s_next = scores(0)
    for hd in range(ATTN_HEADS):
        cols = slice(hd * V_DIM, (hd + 1) * V_DIM)
        s = s_next
        if hd + 1 < ATTN_HEADS:
            s_next = scores(hd + 1)
        mx = functools.reduce(jnp.maximum, [jnp.max(p, axis=0, keepdims=True) for p in s])
        e = [jnp.exp(p - mx).astype(BF16) for p in s]
        vrows = slice(hd * VT_HEAD_ROWS, (hd + 1) * VT_HEAD_ROWS)
        o = functools.reduce(jnp.add, [
            jnp.dot(vt[0, vrows, :], p, preferred_element_type=F32) for p, vt in zip(e, vt_refs)])
        o = o[:V_DIM] / o[V_DIM:V_DIM + 1]
        oh = (o[:, :tq] - lam * o[:, tq:]).T
        attn_scr[:, cols] = (_rms(oh, subg) * (1.0 - lam_init)).astype(BF16)

    a = jnp.dot(attn_scr[...], wa_ref[...], preferred_element_type=F32)
    merged = ga_ref[0].astype(F32) * a + sgp_ref[0].astype(F32)
    mix = jnp.dot(merged.astype(BF16), wo_ref[...], preferred_element_type=F32)
    gate = mod_ref[0][:, 2 * D_MODEL:3 * D_MODEL]
    o_ref[0] = x_ref[0] + gate * mix


def _attention(qt, ks, vts, ga, sgp, x, mod, per_batch_mod, lam_vecs, subg, w_a, w_o, lam_init, tq):
    batch, seq, _ = x.shape
    n_pieces = len(ks)
    mod_map = (lambda b, s: (b, 0, 0)) if per_batch_mod else (lambda b, s: (0, 0, 0))
    tok = lambda width: pl.BlockSpec((1, tq, width), lambda b, s: (b, s, 0))
    in_specs = [pl.BlockSpec((1, QK_WIDTH, tq), lambda b, s: (b, 0, s))]
    in_specs += [pl.BlockSpec((1, k.shape[1], QK_WIDTH), lambda b, s: (b, 0, 0)) for k in ks]
    in_specs += [pl.BlockSpec((1, VT_ROWS, vt.shape[2]), lambda b, s: (b, 0, 0)) for vt in vts]
    in_specs += [tok(D_MODEL), tok(D_MODEL), tok(D_MODEL),
                 pl.BlockSpec((1, 1, 6 * D_MODEL), mod_map)]
    in_specs += [_const_spec((1, QK_DIM))] * 4
    in_specs += [_const_spec((1, V_DIM)), _const_spec((ATTN_WIDTH, D_MODEL)),
                 _const_spec((D_MODEL, D_MODEL))]
    return pl.pallas_call(
        functools.partial(_attn_kernel, n_pieces=n_pieces, lam_init=lam_init),
        grid=(batch, seq // tq),
        in_specs=in_specs,
        out_specs=tok(D_MODEL),
        out_shape=jax.ShapeDtypeStruct((batch, seq, D_MODEL), F32),
        scratch_shapes=[pltpu.VMEM((tq, ATTN_WIDTH), BF16)],
        compiler_params=_params(2),
        name="diff_attention_merge",
    )(qt, *ks, *vts, ga, sgp, x, mod, *lam_vecs, subg, w_a, w_o)


def _ffn_kernel(*refs, final):
    if final:
        x_ref, mod_ref, g_ref, w1_ref, w2_ref, fg_ref, o_ref = refs
    else:
        x_ref, mod_ref, g_ref, w1_ref, w2_ref, o_ref = refs
    x = x_ref[0]
    m = mod_ref[0]
    shift, scale, gate = (m[:, 3 * D_MODEL:4 * D_MODEL], m[:, 4 * D_MODEL:5 * D_MODEL],
                          m[:, 5 * D_MODEL:6 * D_MODEL])
    h = (_rms(x, g_ref[...]) * (1.0 + scale) + shift).astype(BF16)
    a = jnp.dot(h, w1_ref[:, :D_FF], preferred_element_type=F32)
    b = jnp.dot(h, w1_ref[:, D_FF:], preferred_element_type=F32)
    f = (jax.nn.silu(a) * b).astype(BF16)
    y = x + gate * jnp.dot(f, w2_ref[...], preferred_element_type=F32)
    if final:
        y = _rms(y, fg_ref[...])
    o_ref[0] = y


def _ffn(x, mod, per_batch_mod, norm_g, w1, w2, final_g, tm):
    batch, seq, _ = x.shape
    final = final_g is not None
    mod_map = (lambda b, s: (b, 0, 0)) if per_batch_mod else (lambda b, s: (0, 0, 0))
    tok = pl.BlockSpec((1, tm, D_MODEL), lambda b, s: (b, s, 0))
    in_specs = [tok, pl.BlockSpec((1, 1, 6 * D_MODEL), mod_map), _const_spec((1, D_MODEL)),
                _const_spec((D_MODEL, 2 * D_FF)), _const_spec((D_FF, D_MODEL))]
    args = [x, mod, norm_g, w1, w2]
    if final:
        in_specs.append(_const_spec((1, D_MODEL)))
        args.append(final_g)
    return pl.pallas_call(
        functools.partial(_ffn_kernel, final=final),
        grid=(batch, seq // tm),
        in_specs=in_specs,
        out_specs=tok,
        out_shape=jax.ShapeDtypeStruct((batch, seq, D_MODEL), F32),
        compiler_params=_params(2),
        name="swiglu_ffn",
    )(*args)


def _rope_tables(n_tokens):
    pos = jnp.arange(n_tokens)
    row = (pos // GRID_W).astype(F32)
    col = (pos % GRID_W).astype(F32)
    half = QK_DIM // 2
    inv = 1.0 / (ROPE_THETA ** (jnp.arange(0, half, 2, dtype=F32) / half))
    ang_r = row[:, None] * inv[None, :]
    ang_c = col[:, None] * inv[None, :]
    reps = LANES // QK_DIM
    cos = jnp.tile(jnp.concatenate([jnp.cos(ang_r)] * 2 + [jnp.cos(ang_c)] * 2, axis=1), (1, reps))
    sin = jnp.tile(jnp.concatenate([jnp.sin(ang_r)] * 2 + [jnp.sin(ang_c)] * 2, axis=1), (1, reps))
    first_half = (jnp.arange(LANES) % (2 * ROPE_HALF)) < ROPE_HALF
    sin_lo = jnp.where(first_half[None, :], -sin, 0.0)
    sin_hi = jnp.where(first_half[None, :], 0.0, sin)
    return cos, sin_lo, sin_hi


def kernel(x, c, ctx, c_ctx, ada_w, ada_b, norm1_g, w_in, lambda_q1, lambda_k1, lambda_q2,
           lambda_k2, subln_g, sg_norm_g, sg_w, sg_b, w_branch_attn, w_branch_sg, w_out,
           norm2_g, w_ffn_in, w_ffn_out, final_g):
    batch, seq, _ = x.shape
    ctx_len = ctx.shape[1]
    tm_lat, tq_lat = 512, 512
    tm_ctx = ctx_len

    pad = (-(batch + 1)) % 8
    c_all = jnp.concatenate([c, c_ctx[None, :], jnp.zeros((pad, D_MODEL), F32)], axis=0)
    mod_all = _modulation(c_all, ada_w, ada_b)
    tables = _rope_tables(seq)

    w_in_b = w_in.astype(BF16)
    sgw_b = sg_w.astype(BF16)
    wsg_b = w_branch_sg.astype(BF16)
    wa_b = w_branch_attn.astype(BF16)
    wo_b = w_out.astype(BF16)
    w1_b = w_ffn_in.astype(BF16)
    w2_b = w_ffn_out.astype(BF16)

    x_l, x_c = x, ctx
    for i in range(DEPTH):
        last = i == DEPTH - 1
        lam_init = 0.8 - 0.6 * math.exp(-0.3 * i)
        mod_l = mod_all[i, :batch].reshape(batch, 1, 6 * D_MODEL)
        mod_c = mod_all[i, batch:batch + 1].reshape(1, 1, 6 * D_MODEL)
        n1 = norm1_g[i].reshape(1, D_MODEL)
        n2 = norm2_g[i].reshape(1, D_MODEL)
        sg_params = (sg_norm_g[i].reshape(1, SG_WIDTH), sgw_b[i], sg_b[i].T, wsg_b[i])
        lam_vecs = [v[i].reshape(1, QK_DIM) for v in (lambda_q1, lambda_k1, lambda_q2, lambda_k2)]
        subg = subln_g[i].reshape(1, V_DIM)

        if last:
            k_c, vt_c = _in_projection(x_c, mod_c, False, n1, w_in_b[i], None, None, tm_ctx, True)
        else:
            qt_c, k_c, vt_c, ga_c, sgp_c = _in_projection(
                x_c, mod_c, False, n1, w_in_b[i], None, sg_params, tm_ctx, False)
        qt_l, k_l, vt_l, ga_l, sgp_l = _in_projection(
            x_l, mod_l, True, n1, w_in_b[i], tables, sg_params, tm_lat, False)

        x_l = _attention(qt_l, [k_c, k_l], [vt_c, vt_l], ga_l, sgp_l, x_l, mod_l, True,
                         lam_vecs, subg, wa_b[i], wo_b[i], lam_init, tq_lat)
        x_l = _ffn(x_l, mod_l, True, n2, w1_b[i], w2_b[i],
                   final_g.reshape(1, D_MODEL) if last else None, tm_lat)
        if not last:
            x_c = _attention(qt_c, [k_c], [vt_c], ga_c, sgp_c, x_c, mod_c, False,
                             lam_vecs, subg, wa_b[i], wo_b[i], lam_init, tm_ctx)
            x_c = _ffn(x_c, mod_c, False, n2, w1_b[i], w2_b[i], None, tm_ctx)
    return x_l
```

```python
import functools
import math

import jax
import jax.numpy as jnp
from jax import lax
from jax.experimental import pallas as pl
from jax.experimental.pallas import tpu as pltpu

D_MODEL = 1024
DEPTH = 4
GRID_W = 64
ATTN_HEADS = 4
QK_DIM = 64
V_DIM = 2 * QK_DIM
ATTN_WIDTH = ATTN_HEADS * V_DIM
QK_WIDTH = ATTN_HEADS * 2 * QK_DIM
CHUNK = 128
SG_GROUPS = 4
SG_GROUP_DIM = 128
SG_WIDTH = SG_GROUPS * SG_GROUP_DIM
D_FF = ((8 * D_MODEL // 3 + 255) // 256) * 256
ROPE_THETA = 10000.0
EPS = 1e-6

K_OFF = 0
V_OFF = K_OFF + QK_WIDTH
Q_OFF = V_OFF + ATTN_WIDTH
U_OFF = Q_OFF + QK_WIDTH
SGV_OFF = U_OFF + SG_WIDTH
GATE_OFF = SGV_OFF + SG_WIDTH
IN_WIDTH = GATE_OFF + 2 * D_MODEL

LANES = 128
ROPE_HALF = QK_DIM // 4
BF16_SUBLANES = 16
VT_HEAD_ROWS = V_DIM + BF16_SUBLANES
VT_ROWS = ATTN_HEADS * VT_HEAD_ROWS
VMEM_LIMIT_BYTES = 56 * 1024 * 1024
Q_SCALE = QK_DIM ** -0.5 * math.log2(math.e)

F32 = jnp.float32
BF16 = jnp.bfloat16


def _const_spec(shape):
    zeros = (0,) * len(shape)
    return pl.BlockSpec(shape, lambda *_: zeros, pipeline_mode=pl.Buffered(1))


def _layer_spec(shape, layer):
    zeros = (0,) * len(shape)
    return pl.BlockSpec((None,) + tuple(shape), lambda *_: (layer,) + zeros,
                        pipeline_mode=pl.Buffered(1))


def _params(n_grid_dims):
    return pltpu.CompilerParams(
        dimension_semantics=("arbitrary",) * n_grid_dims,
        vmem_limit_bytes=VMEM_LIMIT_BYTES)


def _rms(x, g):
    return x * lax.rsqrt(jnp.mean(x * x, axis=-1, keepdims=True) + EPS) * g


def _mod_kernel(c_ref, w_ref, b_ref, o_ref):
    sc = jax.nn.silu(c_ref[...]).astype(BF16)
    o_ref[0] = jnp.dot(sc, w_ref[0].astype(BF16), preferred_element_type=F32) + b_ref[0]


def _modulation(c_all, ada_w, ada_b):
    rows = c_all.shape[0]
    n_col = 6
    return pl.pallas_call(
        _mod_kernel,
        grid=(DEPTH, n_col),
        in_specs=[
            pl.BlockSpec((rows, D_MODEL), lambda i, j: (0, 0)),
            pl.BlockSpec((1, D_MODEL, D_MODEL), lambda i, j: (i, 0, j)),
            pl.BlockSpec((1, 1, D_MODEL), lambda i, j: (i, 0, j)),
        ],
        out_specs=pl.BlockSpec((1, rows, D_MODEL), lambda i, j: (i, 0, j)),
        out_shape=jax.ShapeDtypeStruct((DEPTH, rows, 6 * D_MODEL), F32),
        compiler_params=_params(2),
        name="adaln_modulation",
    )(c_all, ada_w, ada_b.reshape(DEPTH, 1, 6 * D_MODEL))


def _rope(p, cos, sin_lo, sin_hi):
    out = []
    for j in range(p.shape[1] // LANES):
        blk = p[:, j * LANES:(j + 1) * LANES]
        nxt = pltpu.roll(blk, LANES - ROPE_HALF, axis=1)
        prv = pltpu.roll(blk, ROPE_HALF, axis=1)
        out.append(blk * cos + nxt * sin_lo + prv * sin_hi)
    return jnp.concatenate(out, axis=1)


def _inproj_kernel(*refs, rope, kv_only):
    x_ref, mod_ref, g_ref, w_ref = refs[:4]
    refs = refs[4:]
    if rope:
        cos_ref, slo_ref, shi_ref = refs[:3]
        refs = refs[3:]
    if kv_only:
        k_ref, vt_ref = refs
    else:
        lng_ref, sgw_ref, sgb_ref, wsg_ref, qt_ref, k_ref, vt_ref, ga_ref, sgp_ref, sg_scr = refs

    x = x_ref[0]
    m = mod_ref[0]
    shift, scale = m[:, 0:D_MODEL], m[:, D_MODEL:2 * D_MODEL]
    h = (_rms(x, g_ref[...]) * (1.0 + scale) + shift).astype(BF16)

    def proj(lo, hi):
        return jnp.dot(h, w_ref[:, lo:hi], preferred_element_type=F32)

    if rope:
        cos, slo, shi = cos_ref[...], slo_ref[...], shi_ref[...]

    if not kv_only:
        u = jax.nn.gelu(proj(U_OFF, SGV_OFF))
        sv = jax.nn.gelu(proj(SGV_OFF, GATE_OFF))
        mu = jnp.mean(sv, axis=-1, keepdims=True)
        svc = sv - mu
        var = jnp.mean(svc * svc, axis=-1, keepdims=True)
        svn = (svc * lax.rsqrt(var + EPS) * lng_ref[...]).astype(BF16)

    pk = proj(K_OFF, V_OFF)
    if rope:
        pk = _rope(pk, cos, slo, shi)
    k_ref[0] = pk.astype(BF16)
    pvt = proj(V_OFF, Q_OFF).T
    ones = jnp.ones((VT_HEAD_ROWS - V_DIM, x.shape[0]), BF16)
    for hd in range(ATTN_HEADS):
        base = hd * VT_HEAD_ROWS
        vt_ref[0, base:base + V_DIM, :] = pvt[hd * V_DIM:(hd + 1) * V_DIM].astype(BF16)
        vt_ref[0, base + V_DIM:base + VT_HEAD_ROWS, :] = ones
    if kv_only:
        return

    pq = proj(Q_OFF, U_OFF)
    if rope:
        pq = _rope(pq, cos, slo, shi)
    qt_ref[0] = (pq * Q_SCALE).T.astype(BF16)

    ga_ref[0] = jax.nn.sigmoid(proj(GATE_OFF, GATE_OFF + D_MODEL)).astype(BF16)

    sgb = sgb_ref[...]
    n_chunks = x.shape[0] // CHUNK
    for g in range(SG_GROUPS):
        cols = slice(g * SG_GROUP_DIM, (g + 1) * SG_GROUP_DIM)
        rhs = jnp.concatenate([svn[c * CHUNK:(c + 1) * CHUNK, cols] for c in range(n_chunks)], axis=1)
        mixed = jnp.dot(sgw_ref[g], rhs, preferred_element_type=F32) + sgb[:, g:g + 1]
        for c in range(n_chunks):
            rows = slice(c * CHUNK, (c + 1) * CHUNK)
            blk = mixed[:, c * SG_GROUP_DIM:(c + 1) * SG_GROUP_DIM]
            sg_scr[rows, cols] = (u[rows, cols] * blk).astype(BF16)

    gb = jax.nn.sigmoid(proj(GATE_OFF + D_MODEL, IN_WIDTH))
    sgp = jnp.dot(sg_scr[...], wsg_ref[...], preferred_element_type=F32)
    sgp_ref[0] = (gb * sgp).astype(BF16)


def _in_projection(x, mod, per_batch_mod, norm_g, w_in, layer, rope_tables, sg_params, tm, kv_only):
    batch, seq, _ = x.shape
    rope = rope_tables is not None
    mod_map = (lambda s, b: (b, 0, 0)) if per_batch_mod else (lambda s, b: (0, 0, 0))
    w_cols = Q_OFF if kv_only else IN_WIDTH
    in_specs = [
        pl.BlockSpec((1, tm, D_MODEL), lambda s, b: (b, s, 0)),
        pl.BlockSpec((1, 1, 6 * D_MODEL), mod_map),
        _const_spec((1, D_MODEL)),
        _layer_spec((D_MODEL, w_cols), layer),
    ]
    args = [x, mod, norm_g, w_in]
    if rope:
        in_specs += [pl.BlockSpec((tm, LANES), lambda s, b: (s, 0))] * 3
        args += list(rope_tables)
    tok = lambda width: pl.BlockSpec((1, tm, width), lambda s, b: (b, s, 0))
    tok_shape = lambda width: jax.ShapeDtypeStruct((batch, seq, width), BF16)
    tr = lambda rows: pl.BlockSpec((1, rows, tm), lambda s, b: (b, 0, s))
    tr_shape = lambda rows: jax.ShapeDtypeStruct((batch, rows, seq), BF16)
    if kv_only:
        out_specs = [tok(QK_WIDTH), tr(VT_ROWS)]
        out_shape = [tok_shape(QK_WIDTH), tr_shape(VT_ROWS)]
        scratch = []
    else:
        lng, sgw, sgb_t, wsg = sg_params
        in_specs += [
            _const_spec((1, SG_WIDTH)),
            _layer_spec((SG_GROUPS, CHUNK, CHUNK), layer),
            _const_spec((CHUNK, SG_GROUPS)),
            _layer_spec((SG_WIDTH, D_MODEL), layer),
        ]
        args += [lng, sgw, sgb_t, wsg]
        out_specs = [tr(QK_WIDTH), tok(QK_WIDTH), tr(VT_ROWS), tok(D_MODEL), tok(D_MODEL)]
        out_shape = [tr_shape(QK_WIDTH), tok_shape(QK_WIDTH), tr_shape(VT_ROWS),
                     tok_shape(D_MODEL), tok_shape(D_MODEL)]
        scratch = [pltpu.VMEM((tm, SG_WIDTH), BF16)]
    return pl.pallas_call(
        functools.partial(_inproj_kernel, rope=rope, kv_only=kv_only),
        grid=(seq // tm, batch),
        in_specs=in_specs,
        out_specs=out_specs,
        out_shape=out_shape,
        scratch_shapes=scratch,
        compiler_params=_params(2),
        name="in_projection",
    )(*args)


def _lambda(lq1_ref, lk1_ref, lq2_ref, lk2_ref, lam_init):
    return (jnp.exp(jnp.sum(lq1_ref[...] * lk1_ref[...], axis=-1, keepdims=True))
            - jnp.exp(jnp.sum(lq2_ref[...] * lk2_ref[...], axis=-1, keepdims=True))
            + lam_init)


def _score_weights(q_ref, hd):
    qt = q_ref[0, hd * V_DIM:(hd + 1) * V_DIM, :]
    first_map = lax.broadcasted_iota(jnp.int32, (V_DIM, 1), 0) < QK_DIM
    zero = jnp.zeros_like(qt)
    return jnp.concatenate([jnp.where(first_map, qt, zero), jnp.where(first_map, zero, qt)], axis=1)


def _head_output(o, lam, subg, lam_init, tq):
    o = o[:V_DIM] / o[V_DIM:V_DIM + 1]
    oh = (o[:, :tq] - lam * o[:, tq:]).T
    return (_rms(oh, subg) * (1.0 - lam_init)).astype(BF16)


def _attn_kernel(*refs, n_pieces, lam_init):
    qt_ref = refs[0]
    k_refs = refs[1:1 + n_pieces]
    vt_refs = refs[1 + n_pieces:1 + 2 * n_pieces]
    lq1_ref, lk1_ref, lq2_ref, lk2_ref, subg_ref, o_ref = refs[1 + 2 * n_pieces:]
    tq = qt_ref.shape[2]
    lam = _lambda(lq1_ref, lk1_ref, lq2_ref, lk2_ref, lam_init)
    subg = subg_ref[...]

    def scores(hd):
        w = _score_weights(qt_ref, hd)
        return [jnp.dot(k[0, :, hd * V_DIM:(hd + 1) * V_DIM], w, preferred_element_type=F32)
                for k in k_refs]

    s_next = scores(0)
    for hd in range(ATTN_HEADS):
        s = s_next
        if hd + 1 < ATTN_HEADS:
            s_next = scores(hd + 1)
        mx = functools.reduce(jnp.maximum, [jnp.max(p, axis=0, keepdims=True) for p in s])
        e = [jnp.exp2(p - mx).astype(BF16) for p in s]
        vrows = slice(hd * VT_HEAD_ROWS, (hd + 1) * VT_HEAD_ROWS)
        o = functools.reduce(jnp.add, [
            jnp.dot(vt[0, vrows, :], p, preferred_element_type=F32) for p, vt in zip(e, vt_refs)])
        o_ref[0, :, hd * V_DIM:(hd + 1) * V_DIM] = _head_output(o, lam, subg, lam_init, tq)


def _attention(qt, ks, vts, lam_vecs, subg, lam_init, tq):
    batch, _, seq = qt.shape
    in_specs = [pl.BlockSpec((1, QK_WIDTH, tq), lambda b, s: (b, 0, s))]
    in_specs += [pl.BlockSpec((1, k.shape[1], QK_WIDTH), lambda b, s: (b, 0, 0)) for k in ks]
    in_specs += [pl.BlockSpec((1, VT_ROWS, vt.shape[2]), lambda b, s: (b, 0, 0)) for vt in vts]
    in_specs += [_const_spec((1, QK_DIM))] * 4
    in_specs += [_const_spec((1, V_DIM))]
    return pl.pallas_call(
        functools.partial(_attn_kernel, n_pieces=len(ks), lam_init=lam_init),
        grid=(batch, seq // tq),
        in_specs=in_specs,
        out_specs=pl.BlockSpec((1, tq, ATTN_WIDTH), lambda b, s: (b, s, 0)),
        out_shape=jax.ShapeDtypeStruct((batch, seq, ATTN_WIDTH), BF16),
        compiler_params=_params(2),
        name="diff_attention",
    )(qt, *ks, *vts, *lam_vecs, subg)


def _ffn_kernel(*refs, final):
    if final:
        (x_ref, attn_ref, ga_ref, sgp_ref, mod_ref, g_ref, wa_ref, wo_ref, w1_ref, w2_ref,
         fg_ref, o_ref) = refs
    else:
        (x_ref, attn_ref, ga_ref, sgp_ref, mod_ref, g_ref, wa_ref, wo_ref, w1_ref, w2_ref,
         o_ref) = refs
    m = mod_ref[0]
    gate1 = m[:, 2 * D_MODEL:3 * D_MODEL]
    shift, scale, gate = (m[:, 3 * D_MODEL:4 * D_MODEL], m[:, 4 * D_MODEL:5 * D_MODEL],
                          m[:, 5 * D_MODEL:6 * D_MODEL])
    a = jnp.dot(attn_ref[0], wa_ref[...], preferred_element_type=F32)
    merged = ga_ref[0].astype(F32) * a + sgp_ref[0].astype(F32)
    mix = jnp.dot(merged.astype(BF16), wo_ref[...], preferred_element_type=F32)
    x = x_ref[0] + gate1 * mix
    h = (_rms(x, g_ref[...]) * (1.0 + scale) + shift).astype(BF16)
    a = jnp.dot(h, w1_ref[:, :D_FF], preferred_element_type=F32)
    b = jnp.dot(h, w1_ref[:, D_FF:], preferred_element_type=F32)
    f = (jax.nn.silu(a) * b).astype(BF16)
    y = x + gate * jnp.dot(f, w2_ref[...], preferred_element_type=F32)
    if final:
        y = _rms(y, fg_ref[...])
    o_ref[0] = y


def _merge_ffn(x, attn, ga, sgp, mod, per_batch_mod, norm_g, w_a, w_o, w1, w2, layer, final_g, tm):
    batch, seq, _ = x.shape
    final = final_g is not None
    mod_map = (lambda b, s: (b, 0, 0)) if per_batch_mod else (lambda b, s: (0, 0, 0))
    tok = lambda width: pl.BlockSpec((1, tm, width), lambda b, s: (b, s, 0))
    in_specs = [tok(D_MODEL), tok(ATTN_WIDTH), tok(D_MODEL), tok(D_MODEL),
                pl.BlockSpec((1, 1, 6 * D_MODEL), mod_map), _const_spec((1, D_MODEL)),
                _layer_spec((ATTN_WIDTH, D_MODEL), layer), _layer_spec((D_MODEL, D_MODEL), layer),
                _layer_spec((D_MODEL, 2 * D_FF), layer), _layer_spec((D_FF, D_MODEL), layer)]
    args = [x, attn, ga, sgp, mod, norm_g, w_a, w_o, w1, w2]
    if final:
        in_specs.append(_const_spec((1, D_MODEL)))
        args.append(final_g)
    return pl.pallas_call(
        functools.partial(_ffn_kernel, final=final),
        grid=(batch, seq // tm),
        in_specs=in_specs,
        out_specs=tok(D_MODEL),
        out_shape=jax.ShapeDtypeStruct((batch, seq, D_MODEL), F32),
        compiler_params=_params(2),
        name="merge_ffn",
    )(*args)


def _rope_tables(n_tokens):
    pos = jnp.arange(n_tokens)
    row = (pos // GRID_W).astype(F32)
    col = (pos % GRID_W).astype(F32)
    half = QK_DIM // 2
    inv = 1.0 / (ROPE_THETA ** (jnp.arange(0, half, 2, dtype=F32) / half))
    ang_r = row[:, None] * inv[None, :]
    ang_c = col[:, None] * inv[None, :]
    reps = LANES // QK_DIM
    cos = jnp.tile(jnp.concatenate([jnp.cos(ang_r)] * 2 + [jnp.cos(ang_c)] * 2, axis=1), (1, reps))
    sin = jnp.tile(jnp.concatenate([jnp.sin(ang_r)] * 2 + [jnp.sin(ang_c)] * 2, axis=1), (1, reps))
    first_half = (jnp.arange(LANES) % (2 * ROPE_HALF)) < ROPE_HALF
    sin_lo = jnp.where(first_half[None, :], -sin, 0.0)
    sin_hi = jnp.where(first_half[None, :], 0.0, sin)
    return cos, sin_lo, sin_hi


def kernel(x, c, ctx, c_ctx, ada_w, ada_b, norm1_g, w_in, lambda_q1, lambda_k1, lambda_q2,
           lambda_k2, subln_g, sg_norm_g, sg_w, sg_b, w_branch_attn, w_branch_sg, w_out,
           norm2_g, w_ffn_in, w_ffn_out, final_g):
    batch, seq, _ = x.shape
    ctx_len = ctx.shape[1]
    tm_lat, tq_lat = 512, 512
    tm_ctx = ctx_len

    pad = (-(batch + 1)) % 8
    c_all = jnp.concatenate([c, c_ctx[None, :], jnp.zeros((pad, D_MODEL), F32)], axis=0)
    mod_all = _modulation(c_all, ada_w, ada_b)
    tables = _rope_tables(seq)

    w_in_b = w_in.astype(BF16)
    sgw_b = sg_w.astype(BF16)
    wsg_b = w_branch_sg.astype(BF16)
    wa_b = w_branch_attn.astype(BF16)
    wo_b = w_out.astype(BF16)
    w1_b = w_ffn_in.astype(BF16)
    w2_b = w_ffn_out.astype(BF16)

    x_l, x_c = x, ctx
    for i in range(DEPTH):
        last = i == DEPTH - 1
        lam_init = 0.8 - 0.6 * math.exp(-0.3 * i)
        mod_l = mod_all[i, :batch].reshape(batch, 1, 6 * D_MODEL)
        mod_c = mod_all[i, batch:batch + 1].reshape(1, 1, 6 * D_MODEL)
        n1 = norm1_g[i].reshape(1, D_MODEL)
        n2 = norm2_g[i].reshape(1, D_MODEL)
        sg_params = (sg_norm_g[i].reshape(1, SG_WIDTH), sgw_b, sg_b[i].T, wsg_b)
        lam_vecs = [v[i].reshape(1, QK_DIM) for v in (lambda_q1, lambda_k1, lambda_q2, lambda_k2)]
        subg = subln_g[i].reshape(1, V_DIM)

        if last:
            k_c, vt_c = _in_projection(x_c, mod_c, False, n1, w_in_b, i, None, None, tm_ctx, True)
        else:
            qt_c, k_c, vt_c, ga_c, sgp_c = _in_projection(
                x_c, mod_c, False, n1, w_in_b, i, None, sg_params, tm_ctx, False)
        qt_l, k_l, vt_l, ga_l, sgp_l = _in_projection(
            x_l, mod_l, True, n1, w_in_b, i, tables, sg_params, tm_lat, False)

        attn_l = _attention(qt_l, [k_c, k_l], [vt_c, vt_l], lam_vecs, subg, lam_init, tq_lat)
        x_l = _merge_ffn(x_l, attn_l, ga_l, sgp_l, mod_l, True, n2, wa_b, wo_b, w1_b, w2_b, i,
                         final_g.reshape(1, D_MODEL) if last else None, tm_lat)
        if not last:
            attn_c = _attention(qt_c, [k_c], [vt_c], lam_vecs, subg, lam_init, tm_ctx)
            x_c = _merge_ffn(x_c, attn_c, ga_c, sgp_c, mod_c, False, n2, wa_b, wo_b, w1_b, w2_b, i,
                             None, tm_ctx)
    return x_l
```
